```python
import jax, jax.numpy as jnp
from jax import lax
import numpy as np

D_MODEL = 1024
BATCH = 1
SEQ = 16384
DEPTH = 4

GLA_HEADS = 4
GLA_DK_HEAD = 64
GLA_DV_HEAD = 128
GLA_DK = GLA_HEADS * GLA_DK_HEAD
GLA_DV = GLA_HEADS * GLA_DV_HEAD
GLA_GATE_RANK = 16
GLA_GATE_TAU = 16.0
GLA_CHUNK = 64
SWA_Q_HEADS = 8
SWA_KV_HEADS = 2
SWA_HEAD_DIM = 64
SWA_WINDOW = 128
SWA_BLOCK = 128
SWA_Q_DIM = SWA_Q_HEADS * SWA_HEAD_DIM
SWA_KV_DIM = SWA_KV_HEADS * SWA_HEAD_DIM
POOL_GROUPS = 4
POOL_GROUP_DIM = 128
POOL_DIM = POOL_GROUPS * POOL_GROUP_DIM
POOL_WINDOWS = (2, 4, 8, 16)
N_BRANCHES = 3
D_FF = 2816
CONV_WIDTH = 3
RMS_EPS = 1e-6

SPLIT_SIZES = (GLA_DK, GLA_DK, GLA_DV, GLA_GATE_RANK, GLA_DV,
               SWA_Q_DIM, SWA_KV_DIM, SWA_KV_DIM,
               POOL_DIM,
               N_BRANCHES * D_MODEL)
N_IN = sum(SPLIT_SIZES)
SPLIT_POINTS = tuple(int(v) for v in np.cumsum(SPLIT_SIZES)[:-1])

kernel_name = "hybrid_gla_swa_pool_gated_convffn"


def rmsnorm(x, gain):
    xf = x.astype(jnp.float32)
    var = jnp.mean(xf * xf, axis=-1, keepdims=True)
    return (xf * lax.rsqrt(var + RMS_EPS) * gain.astype(jnp.float32)).astype(x.dtype)


def gla_mixer(q, k, v, gate_low, r, w_gate_up, b_gate, norm_gain):
    f32 = jnp.float32
    B, T, _ = q.shape
    nc = T // GLA_CHUNK
    logit = gate_low.astype(f32) @ w_gate_up.astype(f32) + b_gate.astype(f32)
    log_alpha = jax.nn.log_sigmoid(logit) / GLA_GATE_TAU

    def to_chunks(t, d):
        return t.astype(f32).reshape(B, nc, GLA_CHUNK, GLA_HEADS, d).transpose(1, 0, 3, 2, 4)

    qc = to_chunks(q, GLA_DK_HEAD) * (GLA_DK_HEAD ** -0.5)
    kc = to_chunks(k, GLA_DK_HEAD)
    vc = to_chunks(v, GLA_DV_HEAD)
    gc = to_chunks(log_alpha, GLA_DK_HEAD)
    causal = jnp.tril(jnp.ones((GLA_CHUNK, GLA_CHUNK), dtype=bool))

    def step(S, inp):
        qi, ki, vi, gi = inp
        b = jnp.cumsum(gi, axis=2)
        diff = b[:, :, :, None, :] - b[:, :, None, :, :]
        decay = jnp.exp(jnp.where(causal[None, None, :, :, None], diff, -jnp.inf))
        attn = jnp.einsum('bhid,bhjd,bhijd->bhij', qi, ki, decay)
        o = jnp.einsum('bhij,bhjv->bhiv', attn, vi) \
            + jnp.einsum('bhid,bhdv->bhiv', qi * jnp.exp(b), S)
        b_last = b[:, :, -1:, :]
        S_new = jnp.exp(b_last[:, :, 0, :])[..., None] * S \
            + jnp.einsum('bhjd,bhjv->bhdv', ki * jnp.exp(b_last - b), vi)
        return S_new, o

    S0 = jnp.zeros((B, GLA_HEADS, GLA_DK_HEAD, GLA_DV_HEAD), f32)
    _, o = lax.scan(step, S0, (qc, kc, vc, gc))
    o = o.transpose(1, 0, 3, 2, 4).reshape(B, T, GLA_HEADS, GLA_DV_HEAD)
    var = jnp.mean(o * o, axis=-1, keepdims=True)
    o = o * lax.rsqrt(var + RMS_EPS) * norm_gain.astype(f32).reshape(GLA_HEADS, GLA_DV_HEAD)
    o = o.reshape(B, T, GLA_DV) * jax.nn.silu(r.astype(f32))
    return o.astype(q.dtype)


def swa_mixer(q, k, v, sinks):
    f32 = jnp.float32
    B, T, _ = q.shape
    nb = T // SWA_BLOCK
    G = SWA_Q_HEADS // SWA_KV_HEADS
    qb = q.astype(f32).reshape(B, nb, SWA_BLOCK, SWA_KV_HEADS, G, SWA_HEAD_DIM) * (SWA_HEAD_DIM ** -0.5)
    kb = k.astype(f32).reshape(B, nb, SWA_BLOCK, SWA_KV_HEADS, SWA_HEAD_DIM)
    vb = v.astype(f32).reshape(B, nb, SWA_BLOCK, SWA_KV_HEADS, SWA_HEAD_DIM)

    def with_prev(t):
        prev = jnp.concatenate([jnp.zeros_like(t[:, :1]), t[:, :-1]], axis=1)
        return jnp.concatenate([prev, t], axis=2)

    kw, vw = with_prev(kb), with_prev(vb)
    s = jnp.einsum('bnqhgd,bnkhd->bnhgqk', qb, kw)
    blk = jnp.arange(nb)[:, None, None] * SWA_BLOCK
    q_pos = blk + jnp.arange(SWA_BLOCK)[None, :, None]
    k_pos = blk - SWA_BLOCK + jnp.arange(2 * SWA_BLOCK)[None, None, :]
    valid = (k_pos <= q_pos) & (q_pos - k_pos < SWA_WINDOW) & (k_pos >= 0)
    s = jnp.where(valid[None, :, None, None], s, -jnp.inf)
    sink = sinks.astype(f32).reshape(1, 1, SWA_KV_HEADS, G, 1, 1)
    m = jnp.maximum(jnp.max(s, axis=-1, keepdims=True), sink)
    p = jnp.exp(s - m)
    denom = jnp.sum(p, axis=-1, keepdims=True) + jnp.exp(sink - m)
    o = jnp.einsum('bnhgqk,bnkhd->bnqhgd', p / denom, vw)
    return o.reshape(B, T, SWA_Q_DIM).astype(q.dtype)


def pool_mixer(u, w_group, scale):
    f32 = jnp.float32
    B, T, _ = u.shape
    uf = u.astype(f32)
    cs = jnp.cumsum(uf, axis=1)
    pos = jnp.arange(1, T + 1, dtype=f32)
    means = []
    for g, w in enumerate(POOL_WINDOWS):
        cg = cs[..., g * POOL_GROUP_DIM:(g + 1) * POOL_GROUP_DIM]
        shifted = jnp.pad(cg, ((0, 0), (w, 0), (0, 0)))[:, :T]
        cnt = jnp.minimum(pos, float(w))
        means.append((cg - shifted) / cnt[None, :, None])
    pooled = jnp.stack(means, axis=2)
    d = pooled - uf.reshape(B, T, POOL_GROUPS, POOL_GROUP_DIM)
    y = jnp.einsum('btgc,gcd->btgd', d, w_group.astype(f32)).reshape(B, T, POOL_DIM)
    return (y * scale.astype(f32)).astype(u.dtype)


def conv_ffn(h, w_up, conv_w, conv_b, w_down):
    T = h.shape[1]
    up = h @ w_up
    padded = jnp.pad(up, ((0, 0), (CONV_WIDTH - 1, 0), (0, 0)))
    conv = conv_b + conv_w[0] * padded[:, 0:T]
    for i in range(1, CONV_WIDTH):
        conv = conv + conv_w[i] * padded[:, i:i + T]
    gate, val = jnp.split(conv, 2, axis=-1)
    return (jax.nn.gelu(gate, approximate=True) * val) @ w_down


def setup_inputs(seed: int = 0) -> dict:
    key = jax.random.key(seed)
    ks = jax.random.split(key, 20)
    L = DEPTH

    def nrm(k, shape, scale):
        return jax.random.normal(k, shape, jnp.float32) * scale

    return {
        "x": nrm(ks[0], (BATCH, SEQ, D_MODEL), 1.0),
        "norm_mix_pre": 1.0 + nrm(ks[1], (L, D_MODEL), 0.05),
        "norm_mix_post": 1.0 + nrm(ks[2], (L, D_MODEL), 0.05),
        "norm_ffn_pre": 1.0 + nrm(ks[3], (L, D_MODEL), 0.05),
        "norm_ffn_post": 1.0 + nrm(ks[4], (L, D_MODEL), 0.05),
        "w_in": nrm(ks[5], (L, D_MODEL, N_IN), D_MODEL ** -0.5),
        "gla_w_gate_up": nrm(ks[6], (L, GLA_GATE_RANK, GLA_DK), GLA_GATE_RANK ** -0.5),
        "gla_b_gate": nrm(ks[7], (L, GLA_DK), 0.1),
        "gla_norm": 1.0 + nrm(ks[8], (L, GLA_DV), 0.05),
        "swa_sinks": nrm(ks[9], (L, SWA_Q_HEADS), 0.5),
        "pool_w": nrm(ks[10], (L, POOL_GROUPS, POOL_GROUP_DIM, POOL_GROUP_DIM), POOL_GROUP_DIM ** -0.5),
        "pool_scale": 1.0 + nrm(ks[11], (L, POOL_DIM), 0.05),
        "w_branch_gla": nrm(ks[12], (L, GLA_DV, D_MODEL), GLA_DV ** -0.5),
        "w_branch_swa": nrm(ks[13], (L, SWA_Q_DIM, D_MODEL), SWA_Q_DIM ** -0.5),
        "w_branch_pool": nrm(ks[14], (L, POOL_DIM, D_MODEL), POOL_DIM ** -0.5),
        "w_out": nrm(ks[15], (L, D_MODEL, D_MODEL), D_MODEL ** -0.5),
        "ffn_w_up": nrm(ks[16], (L, D_MODEL, 2 * D_FF), D_MODEL ** -0.5),
        "ffn_conv_w": nrm(ks[17], (L, CONV_WIDTH, 2 * D_FF), CONV_WIDTH ** -0.5),
        "ffn_conv_b": nrm(ks[18], (L, 2 * D_FF), 0.01),
        "ffn_w_down": nrm(ks[19], (L, D_FF, D_MODEL), D_FF ** -0.5),
    }


def reference(x, norm_mix_pre, norm_mix_post, norm_ffn_pre, norm_ffn_post, w_in,
              gla_w_gate_up, gla_b_gate, gla_norm, swa_sinks, pool_w, pool_scale,
              w_branch_gla, w_branch_swa, w_branch_pool, w_out,
              ffn_w_up, ffn_conv_w, ffn_conv_b, ffn_w_down):
    B, T, _ = x.shape
    for l in range(DEPTH):
        h = rmsnorm(x, norm_mix_pre[l])
        proj = h @ w_in[l]
        (g_q, g_k, g_v, g_low, g_r, s_q, s_k, s_v, p_u, gates) = jnp.split(proj, SPLIT_POINTS, axis=-1)
        y_a = gla_mixer(g_q, g_k, g_v, g_low, g_r, gla_w_gate_up[l], gla_b_gate[l], gla_norm[l]) @ w_branch_gla[l]
        y_b = swa_mixer(s_q, s_k, s_v, swa_sinks[l]) @ w_branch_swa[l]
        y_c = pool_mixer(p_u, pool_w[l], pool_scale[l]) @ w_branch_pool[l]
        gate = jax.nn.sigmoid(gates.reshape(B, T, N_BRANCHES, D_MODEL))
        merged = gate[:, :, 0] * y_a + gate[:, :, 1] * y_b + gate[:, :, 2] * y_c
        x = x + rmsnorm(merged @ w_out[l], norm_mix_post[l])
        h = rmsnorm(x, norm_ffn_pre[l])
        f = conv_ffn(h, ffn_w_up[l], ffn_conv_w[l], ffn_conv_b[l], ffn_w_down[l])
        x = x + rmsnorm(f, norm_ffn_post[l])
    return x
```

```python
import functools

import jax
import jax.numpy as jnp
from jax import lax
from jax.experimental import pallas as pl
from jax.experimental.pallas import tpu as pltpu

D_MODEL = 1024
GLA_HEADS = 4
GLA_DK_HEAD = 64
GLA_DV_HEAD = 128
GLA_DK = GLA_HEADS * GLA_DK_HEAD
GLA_DV = GLA_HEADS * GLA_DV_HEAD
GLA_GATE_RANK = 16
GLA_GATE_TAU = 16.0
SWA_Q_HEADS = 8
SWA_KV_HEADS = 2
SWA_HEAD_DIM = 64
SWA_BLOCK = 128
SWA_Q_DIM = SWA_Q_HEADS * SWA_HEAD_DIM
SWA_KV_DIM = SWA_KV_HEADS * SWA_HEAD_DIM
POOL_GROUPS = 4
POOL_GROUP_DIM = 128
POOL_DIM = POOL_GROUPS * POOL_GROUP_DIM
POOL_WINDOWS = (2, 4, 8, 16)
N_BRANCHES = 3
D_FF = 2816
RMS_EPS = 1e-6

LANES = 128
SUBLANES = 8
GLA_TILE_CHUNK = 128
MAX_TOKEN_TILE = 512
FFN_COL_CHUNK = 256
VMEM_LIMIT_BYTES = 56 * 1024 * 1024

COL_GQ = 0
COL_GK = COL_GQ + GLA_DK
COL_GV = COL_GK + GLA_DK
COL_GR = COL_GV + GLA_DV
COL_SQ = COL_GR + GLA_DV
COL_SK = COL_SQ + SWA_Q_DIM
COL_SV = COL_SK + SWA_KV_DIM
COL_PU = COL_SV + SWA_KV_DIM
COL_GATE = COL_PU + POOL_DIM
COL_GLOW = COL_GATE + N_BRANCHES * D_MODEL
N_IN_PADDED = COL_GLOW + LANES

F32 = jnp.float32
BF16 = jnp.bfloat16


def _rms(x, gain):
    var = jnp.mean(x * x, axis=-1, keepdims=True)
    return x * lax.rsqrt(var + RMS_EPS) * gain


def _sigmoid(x):
    return 0.5 * jnp.tanh(0.5 * x) + 0.5


def _dot(a, b):
    return jnp.dot(a, b, preferred_element_type=F32)


def _dot_nt(a, b):
    return lax.dot_general(a, b, (((1,), (1,)), ((), ())), preferred_element_type=F32)


def _dot_tn(a, b):
    return lax.dot_general(a, b, (((0,), (0,)), ((), ())), preferred_element_type=F32)


def _gla(q, k, v, logit, st_ref, o_ref):
    n = q.shape[0]
    c_len = GLA_TILE_CHUNK
    log_alpha = (jnp.minimum(logit, 0.0) - jnp.log(1.0 + jnp.exp(-jnp.abs(logit)))) * (1.0 / GLA_GATE_TAU)
    la_hi = log_alpha.astype(BF16)
    la_lo = (log_alpha - la_hi.astype(F32)).astype(BF16)

    row = lax.broadcasted_iota(jnp.int32, (c_len, c_len), 0)
    col = lax.broadcasted_iota(jnp.int32, (c_len, c_len), 1)
    causal = row >= col
    tri = jnp.where(causal, 1.0, 0.0).astype(BF16)
    lane_dk = lax.broadcasted_iota(jnp.int32, (c_len, GLA_DK), 1)
    head_masks = [(lane_dk >= h * GLA_DK_HEAD) & (lane_dk < (h + 1) * GLA_DK_HEAD) for h in range(GLA_HEADS)]
    causal_stack = jnp.concatenate([causal] * GLA_HEADS, axis=0)

    for c in range(n // c_len):
        sl = slice(c * c_len, (c + 1) * c_len)
        b = _dot(tri, la_hi[sl]) + _dot(tri, la_lo[sl])
        r = b[c_len // 2 - 1:c_len // 2]
        b_last = b[c_len - 1:c_len]
        qe = q[sl] * jnp.exp(b - r) * (GLA_DK_HEAD ** -0.5)
        ke = (k[sl] * jnp.exp(r - b)).astype(BF16)
        st_r = st_ref[...] * jnp.exp(r)
        q_stack = jnp.concatenate(
            [jnp.where(m, qe, 0.0) for m in head_masks], axis=0).astype(BF16)
        rhs = jnp.concatenate([ke, st_r.astype(BF16)], axis=0)
        res = _dot_nt(q_stack, rhs)
        attn = jnp.where(causal_stack, res[:, :c_len], 0.0).astype(BF16)
        vc = v[sl]
        vc_bf = vc.astype(BF16)
        outs = []
        for h in range(GLA_HEADS):
            rows = slice(h * c_len, (h + 1) * c_len)
            vcols = slice(h * GLA_DV_HEAD, (h + 1) * GLA_DV_HEAD)
            outs.append(_dot(attn[rows], vc_bf[:, vcols]) + res[rows, c_len:])
        o_ref[sl, :] = jnp.concatenate(outs, axis=1)
        upd_all = _dot_tn(vc_bf, ke)
        upd = jnp.zeros((GLA_DV_HEAD, GLA_DK), F32)
        for h in range(GLA_HEADS):
            upd = jnp.where(head_masks[h][:GLA_DV_HEAD], upd_all[h * GLA_DV_HEAD:(h + 1) * GLA_DV_HEAD], upd)
        st_ref[...] = jnp.exp(b_last - r) * (st_r + upd)


def _swa(q, k_all, v_all, sinks_ref, min_key):
    n = q.shape[0]
    blk = SWA_BLOCK
    lane = lax.broadcasted_iota(jnp.int32, (blk, LANES), 1)
    lo = lane < SWA_HEAD_DIM
    qi = lax.broadcasted_iota(jnp.int32, (blk, 2 * blk), 0)
    kc = lax.broadcasted_iota(jnp.int32, (blk, 2 * blk), 1)
    band = (kc > qi) & (kc <= qi + blk)
    n_pairs = SWA_Q_HEADS // 2
    out_blocks = []
    for nb in range(n // blk):
        qb = q[nb * blk:(nb + 1) * blk] * (SWA_HEAD_DIM ** -0.5)
        kw = k_all[nb * blk:(nb + 2) * blk]
        vw = v_all[nb * blk:(nb + 2) * blk]
        parts = []
        for half in (True, False):
            for p in range(n_pairs):
                qt = qb[:, p * LANES:(p + 1) * LANES]
                parts.append(jnp.where(lo if half else ~lo, qt, 0.0))
        lhs = jnp.concatenate(parts, axis=0).astype(BF16)
        s = _dot_nt(lhs, kw)
        valid = band & (kc >= min_key) if nb == 0 else band
        heads = []
        for j in range(SWA_Q_HEADS):
            sj = jnp.where(valid, s[j * blk:(j + 1) * blk], -jnp.inf)
            sink = sinks_ref[j]
            m = jnp.maximum(jnp.max(sj, axis=-1, keepdims=True), sink)
            p = jnp.exp(sj - m)
            denom = jnp.sum(p, axis=-1, keepdims=True) + jnp.exp(sink - m)
            heads.append(_dot(p.astype(BF16), vw) / denom)
        tiles = [jnp.where(lo, heads[p], heads[p + n_pairs]) for p in range(n_pairs)]
        out_blocks.append(jnp.concatenate(tiles, axis=1))
    return jnp.concatenate(out_blocks, axis=0)


def _pool(u, u_bf, u_prev, poolw_ref, pscale, tile_start):
    n = u.shape[0]
    blk = LANES
    t = lax.broadcasted_iota(jnp.int32, (blk, blk), 0)
    s = lax.broadcasted_iota(jnp.int32, (blk, blk), 1)
    pos = lax.broadcasted_iota(jnp.int32, (n, 1), 0) + tile_start + 1
    ys = []
    for g, w in enumerate(POOL_WINDOWS):
        cols = slice(g * POOL_GROUP_DIM, (g + 1) * POOL_GROUP_DIM)
        win_cur = jnp.where((s <= t) & (s > t - w), 1.0, 0.0).astype(BF16)
        win_prev = jnp.where(s - blk > t - w, 1.0, 0.0).astype(BF16)
        blocks = [u_prev[:, cols]] + [u_bf[nb * blk:(nb + 1) * blk, cols] for nb in range(n // blk)]
        sums = jnp.concatenate(
            [_dot(win_prev, blocks[nb]) + _dot(win_cur, blocks[nb + 1]) for nb in range(n // blk)], axis=0)
        cnt = jnp.minimum(pos, w).astype(F32)
        d = sums / cnt - u[:, cols]
        ys.append(_dot(d.astype(BF16), poolw_ref[g]))
    return jnp.concatenate(ys, axis=1) * pscale


def _mixer_kernel(sinks_ref, x_ref, gpre_ref, gpost_ref, wa_ref, wgu_ref, bg_ref, gnorm_ref,
                  poolw_ref, pscale_ref, wbg_ref, wbs_ref, wbp_ref, wout_ref,
                  out_ref, st_ref, kprev_ref, vprev_ref, uprev_ref, ogla_ref):
    i = pl.program_id(0)
    n = x_ref.shape[0]

    @pl.when(i == 0)
    def _():
        st_ref[...] = jnp.zeros_like(st_ref)
        kprev_ref[...] = jnp.zeros_like(kprev_ref)
        vprev_ref[...] = jnp.zeros_like(vprev_ref)
        uprev_ref[...] = jnp.zeros_like(uprev_ref)

    cur_slot = lax.rem(i, 2)
    prev_slot = 1 - cur_slot

    x = x_ref[...]
    h = _rms(x, gpre_ref[...]).astype(BF16)

    def proj(start, width):
        return _dot(h, wa_ref[:, start:start + width])

    g_qk = proj(COL_GQ, 2 * GLA_DK)
    g_v = proj(COL_GV, GLA_DV)
    g_low = proj(COL_GLOW, LANES)
    logit = _dot(g_low.astype(BF16), wgu_ref[...]) + bg_ref[...]
    _gla(g_qk[:, :GLA_DK], g_qk[:, GLA_DK:], g_v, logit, st_ref, ogla_ref)
    g_r = proj(COL_GR, GLA_DV)
    gnorm = gnorm_ref[...]
    o_heads = []
    for hd in range(GLA_HEADS):
        cols = slice(hd * GLA_DV_HEAD, (hd + 1) * GLA_DV_HEAD)
        o_heads.append(_rms(ogla_ref[:, cols], gnorm[:, cols]))
    o_gla = jnp.concatenate(o_heads, axis=1) * (g_r * _sigmoid(g_r))
    y = _sigmoid(proj(COL_GATE, D_MODEL)) * _dot(o_gla.astype(BF16), wbg_ref[...])

    s_q = proj(COL_SQ, SWA_Q_DIM)
    s_kv = proj(COL_SK, 2 * SWA_KV_DIM).astype(BF16)
    k_all = jnp.concatenate([kprev_ref[prev_slot], s_kv[:, :SWA_KV_DIM]], axis=0)
    v_all = jnp.concatenate([vprev_ref[prev_slot], s_kv[:, SWA_KV_DIM:]], axis=0)
    kprev_ref[cur_slot] = s_kv[n - SWA_BLOCK:, :SWA_KV_DIM]
    vprev_ref[cur_slot] = s_kv[n - SWA_BLOCK:, SWA_KV_DIM:]
    o_swa = _swa(s_q, k_all, v_all, sinks_ref, jnp.where(i == 0, SWA_BLOCK, 0))
    y = y + _sigmoid(proj(COL_GATE + D_MODEL, D_MODEL)) * _dot(o_swa.astype(BF16), wbs_ref[...])

    p_u = proj(COL_PU, POOL_DIM)
    p_u_bf = p_u.astype(BF16)
    o_pool = _pool(p_u, p_u_bf, uprev_ref[prev_slot], poolw_ref, pscale_ref[...], i * n)
    uprev_ref[cur_slot] = p_u_bf[n - LANES:]
    y = y + _sigmoid(proj(COL_GATE + 2 * D_MODEL, D_MODEL)) * _dot(o_pool.astype(BF16), wbp_ref[...])

    merged = _dot(y.astype(BF16), wout_ref[...])
    out_ref[...] = x + _rms(merged, gpost_ref[...])


def _ffn_kernel(x_ref, gpre_ref, gpost_ref, wup_ref, cw_ref, cb_ref, wdown_ref,
                out_ref, carry_ref, acc_ref):
    i = pl.program_id(0)
    n = x_ref.shape[0]

    @pl.when(i == 0)
    def _():
        carry_ref[...] = jnp.zeros_like(carry_ref)

    cur_slot = lax.rem(i, 2)
    prev_slot = 1 - cur_slot

    x = x_ref[...]
    h = _rms(x, gpre_ref[...]).astype(BF16)
    cw = cw_ref[...]
    cb = cb_ref[...]
    cc = FFN_COL_CHUNK

    def conv(col0):
        cols = slice(col0, col0 + cc)
        up = _dot(h, wup_ref[:, cols])
        ext = jnp.concatenate([carry_ref[prev_slot, :, cols], up], axis=0)
        carry_ref[cur_slot, :, cols] = up[n - SUBLANES:]
        up_m1 = pltpu.roll(ext, 1, 0)[SUBLANES:]
        up_m2 = pltpu.roll(ext, 2, 0)[SUBLANES:]
        return cb[:, cols] + cw[0:1, cols] * up_m2 + cw[1:2, cols] * up_m1 + cw[2:3, cols] * up

    for j in range(D_FF // cc):
        gate = conv(j * cc)
        val = conv(D_FF + j * cc)
        inner = 0.7978845608028654 * (gate + 0.044715 * (gate * gate * gate))
        act = 0.5 * gate * (1.0 + jnp.tanh(inner))
        part = _dot((act * val).astype(BF16), wdown_ref[j * cc:(j + 1) * cc, :])
        if j == 0:
            acc_ref[...] = part
        else:
            acc_ref[...] += part
    out_ref[...] = x + _rms(acc_ref[...], gpost_ref[...])


def _const_spec(shape):
    zeros = (0,) * len(shape)
    return pl.BlockSpec(shape, lambda i, *_: zeros, pipeline_mode=pl.Buffered(1))


def _mixer_call(x, sinks, gpre, gpost, wa, wgu, bg, gnorm, poolw, pscale, wbg, wbs, wbp, wout):
    t = x.shape[0]
    tt = min(MAX_TOKEN_TILE, t)
    tile = pl.BlockSpec((tt, D_MODEL), lambda i, *_: (i, 0))
    consts = (gpre, gpost, wa, wgu, bg, gnorm, poolw, pscale, wbg, wbs, wbp, wout)
    grid_spec = pltpu.PrefetchScalarGridSpec(
        num_scalar_prefetch=1,
        grid=(t // tt,),
        in_specs=[tile] + [_const_spec(c.shape) for c in consts],
        out_specs=tile,
        scratch_shapes=[
            pltpu.VMEM((GLA_DV_HEAD, GLA_DK), F32),
            pltpu.VMEM((2, SWA_BLOCK, SWA_KV_DIM), BF16),
            pltpu.VMEM((2, SWA_BLOCK, SWA_KV_DIM), BF16),
            pltpu.VMEM((2, LANES, POOL_DIM), BF16),
            pltpu.VMEM((tt, GLA_DV), F32),
        ],
    )
    return pl.pallas_call(
        _mixer_kernel,
        out_shape=jax.ShapeDtypeStruct(x.shape, x.dtype),
        grid_spec=grid_spec,
        compiler_params=pltpu.CompilerParams(
            dimension_semantics=("arbitrary",), vmem_limit_bytes=VMEM_LIMIT_BYTES),
        name="mixer",
    )(sinks, x, *consts)


def _ffn_call(x, gpre, gpost, wup, cw, cb, wdown):
    t = x.shape[0]
    tt = min(MAX_TOKEN_TILE, t)
    tile = pl.BlockSpec((tt, D_MODEL), lambda i: (i, 0))
    consts = (gpre, gpost, wup, cw, cb, wdown)
    return pl.pallas_call(
        _ffn_kernel,
        out_shape=jax.ShapeDtypeStruct(x.shape, x.dtype),
        grid=(t // tt,),
        in_specs=[tile] + [_const_spec(c.shape) for c in consts],
        out_specs=tile,
        scratch_shapes=[
            pltpu.VMEM((2, SUBLANES, 2 * D_FF), F32),
            pltpu.VMEM((tt, D_MODEL), F32),
        ],
        compiler_params=pltpu.CompilerParams(
            dimension_semantics=("arbitrary",), vmem_limit_bytes=VMEM_LIMIT_BYTES),
        name="ffn",
    )(x, *consts)


def _swa_head_order():
    n_pairs = SWA_Q_HEADS // 2
    order = []
    for p in range(n_pairs):
        order += [p, p + n_pairs]
    return order


def _permute_heads(w, axis):
    parts = jnp.split(w, SWA_Q_HEADS, axis=axis)
    return jnp.concatenate([parts[j] for j in _swa_head_order()], axis=axis)


def _prep_w_in(w_in):
    sizes = (GLA_DK, GLA_DK, GLA_DV, GLA_GATE_RANK, GLA_DV, SWA_Q_DIM, SWA_KV_DIM, SWA_KV_DIM,
             POOL_DIM, N_BRANCHES * D_MODEL)
    points = []
    acc = 0
    for s in sizes[:-1]:
        acc += s
        points.append(acc)
    g_q, g_k, g_v, g_low, g_r, s_q, s_k, s_v, p_u, gates = jnp.split(w_in, points, axis=1)
    g_low = jnp.pad(g_low, ((0, 0), (0, LANES - GLA_GATE_RANK)))
    s_q = _permute_heads(s_q, 1)
    return jnp.concatenate([g_q, g_k, g_v, g_r, s_q, s_k, s_v, p_u, gates, g_low], axis=1).astype(BF16)


def kernel(x, norm_mix_pre, norm_mix_post, norm_ffn_pre, norm_ffn_post, w_in, gla_w_gate_up, gla_b_gate, gla_norm, swa_sinks, pool_w, pool_scale, w_branch_gla, w_branch_swa, w_branch_pool, w_out, ffn_w_up, ffn_conv_w, ffn_conv_b, ffn_w_down):
    b, t, d = x.shape
    depth = w_in.shape[0]
    row = lambda a: a.reshape(1, -1).astype(F32)
    outs = []
    for bi in range(b):
        xb = x[bi]
        for l in range(depth):
            wgu = jnp.pad(gla_w_gate_up[l], ((0, LANES - GLA_GATE_RANK), (0, 0))).astype(BF16)
            xb = _mixer_call(
                xb, swa_sinks[l].astype(F32), row(norm_mix_pre[l]), row(norm_mix_post[l]),
                _prep_w_in(w_in[l]), wgu, row(gla_b_gate[l]), row(gla_norm[l]),
                pool_w[l].astype(BF16), row(pool_scale[l]),
                w_branch_gla[l].astype(BF16), _permute_heads(w_branch_swa[l], 0).astype(BF16),
                w_branch_pool[l].astype(BF16), w_out[l].astype(BF16))
            xb = _ffn_call(
                xb, row(norm_ffn_pre[l]), row(norm_ffn_post[l]), ffn_w_up[l].astype(BF16),
                ffn_conv_w[l].astype(F32), row(ffn_conv_b[l]), ffn_w_down[l].astype(BF16))
        outs.append(xb)
    return jnp.stack(outs, axis=0)
```
